```python
import math
import jax, jax.numpy as jnp
from jax import lax
import numpy as np

D_MODEL = 1024
BATCH = 4
SEQ = 4096
DEPTH = 2
DEC_BATCH = 32
DEC_SEQ = 8
PAST_LEN = 8192
PAGE_SIZE = 128

GLA_HEADS = 4
GLA_DK = 64
GLA_DV = 128
GLA_WIDTH = GLA_HEADS * GLA_DV
GLA_GATE_RANK = 16
GLA_TAU = 16.0
GLA_CHUNK = 64
ML_HEADS = 4
ML_DH = 128
ML_WIDTH = ML_HEADS * ML_DH
ML_CONV = 4
ML_CHUNK = 64
MB_HEADS = 8
MB_DH = 64
MB_WIDTH = MB_HEADS * MB_DH
MB_BLOCK = 256
MB_TOPK = 3
MB_QCHUNK = 64
N_BRANCH = 3
EPS = 1e-6
NEG = -1e30
POOL_FACTOR = 1.25

IN_SPLITS = (GLA_HEADS * GLA_DK, GLA_HEADS * GLA_DK, GLA_WIDTH, GLA_GATE_RANK, GLA_WIDTH,
             ML_WIDTH, ML_HEADS, ML_HEADS, ML_WIDTH, ML_WIDTH,
             MB_WIDTH, MB_WIDTH, MB_WIDTH, MB_WIDTH,
             N_BRANCH * D_MODEL)
D_IN = (2 * GLA_HEADS * GLA_DK + 2 * GLA_WIDTH + GLA_GATE_RANK + 3 * ML_WIDTH + 2 * ML_HEADS
        + 4 * MB_WIDTH + N_BRANCH * D_MODEL)

kernel_name = "gla_mlstm_moba_parallel_gated_step"


def rms_norm(x, g):
    xf = x.astype(jnp.float32)
    y = xf * lax.rsqrt(jnp.mean(xf * xf, axis=-1, keepdims=True) + EPS)
    return (y * g.astype(jnp.float32)).astype(x.dtype)


def gla_scan(q, k, v, log_a, s0):
    B, L, H, _ = q.shape
    c = min(GLA_CHUNK, L)
    n = -(-L // c)
    pad = n * c - L

    def prep(a):
        a = jnp.pad(a.astype(jnp.float32), ((0, 0), (0, pad), (0, 0), (0, 0)))
        return a.reshape(B, n, c, H, -1).transpose(1, 0, 3, 2, 4)

    tri = jnp.tril(jnp.ones((c, c), bool))

    def step(S, inp):
        qc, kc, vc, ac = inp
        b = jnp.cumsum(ac, axis=2)
        inter = jnp.einsum('bhtd,bhdv->bhtv', qc * jnp.exp(b), S)
        diff = b[:, :, :, None, :] - b[:, :, None, :, :]
        w = jnp.exp(jnp.where(tri[:, :, None], diff, -jnp.inf))
        A = jnp.einsum('bhtd,bhsd,bhtsd->bhts', qc, kc, w)
        out = inter + jnp.einsum('bhts,bhsv->bhtv', A, vc)
        bl = b[:, :, -1:, :]
        S = (jnp.exp(bl[:, :, 0, :])[..., None] * S
             + jnp.einsum('bhsd,bhsv->bhdv', kc * jnp.exp(bl - b), vc))
        return S, out

    S, o = lax.scan(step, s0.astype(jnp.float32), (prep(q), prep(k), prep(v), prep(log_a)))
    o = o.transpose(1, 0, 3, 2, 4).reshape(B, n * c, H, -1)[:, :L]
    return o, S


def mlstm_scan(q, k, v, ig, lf, C0, n0, m0):
    B, L, H, _ = q.shape
    c = min(ML_CHUNK, L)
    n = -(-L // c)
    pad = n * c - L

    def prep4(a):
        a = jnp.pad(a.astype(jnp.float32), ((0, 0), (0, pad), (0, 0), (0, 0)))
        return a.reshape(B, n, c, H, -1).transpose(1, 0, 3, 2, 4)

    def prep3(a, fill):
        a = jnp.pad(a.astype(jnp.float32), ((0, 0), (0, pad), (0, 0)), constant_values=fill)
        return a.reshape(B, n, c, H).transpose(1, 0, 3, 2)

    tri = jnp.tril(jnp.ones((c, c), bool))

    def step(carry, inp):
        C, nv, m = carry
        qc, kc, vc, ic, fc = inp
        F = jnp.cumsum(fc, axis=-1)
        Dm = jnp.where(tri, F[..., :, None] - F[..., None, :] + ic[..., None, :], NEG)
        lin = F + m[..., None]
        m_t = jnp.maximum(lin, Dm.max(-1))
        W = jnp.exp(Dm - m_t[..., None])
        w_in = jnp.exp(lin - m_t)
        Sqk = jnp.einsum('bhtd,bhsd->bhts', qc, kc) * W
        num = (jnp.einsum('bhts,bhsv->bhtv', Sqk, vc)
               + w_in[..., None] * jnp.einsum('bhvd,bhtd->bhtv', C, qc))
        den = Sqk.sum(-1) + w_in * jnp.einsum('bhd,bhtd->bht', nv, qc)
        h = num / jnp.maximum(jnp.abs(den), jnp.exp(-m_t))[..., None]
        m_new = m_t[..., -1]
        w_end = jnp.exp(F[..., -1:] - F + ic - m_new[..., None])
        decay = jnp.exp(F[..., -1] + m - m_new)
        C = decay[..., None, None] * C + jnp.einsum('bhs,bhsv,bhsd->bhvd', w_end, vc, kc)
        nv = decay[..., None] * nv + jnp.einsum('bhs,bhsd->bhd', w_end, kc)
        return (C, nv, m_new), h

    carry0 = (C0.astype(jnp.float32), n0.astype(jnp.float32), m0.astype(jnp.float32))
    (C, nv, m), h = lax.scan(step, carry0, (prep4(q), prep4(k), prep4(v), prep3(ig, NEG), prep3(lf, 0.0)))
    h = h.transpose(1, 0, 3, 2, 4).reshape(B, n * c, H, -1)[:, :L]
    return h, C, nv, m


def moba_attention(q, k_all, v_all, offset):
    B, Lq, H, dh = q.shape
    T = k_all.shape[1]
    nb = -(-T // MB_BLOCK)
    padk = nb * MB_BLOCK - T
    kb = jnp.pad(k_all, ((0, 0), (0, padk), (0, 0), (0, 0))).reshape(B, nb, MB_BLOCK, H, dh).transpose(0, 3, 1, 2, 4)
    vb = jnp.pad(v_all, ((0, 0), (0, padk), (0, 0), (0, 0))).reshape(B, nb, MB_BLOCK, H, dh).transpose(0, 3, 1, 2, 4)
    kmean = kb.astype(jnp.float32).mean(axis=3)
    k_sel = min(MB_TOPK, nb)
    qc = min(MB_QCHUNK, Lq)
    nq = -(-Lq // qc)
    padq = nq * qc - Lq
    qp = jnp.pad(q, ((0, 0), (0, padq), (0, 0), (0, 0))).reshape(B, nq, qc, H, dh)
    qp = qp.transpose(0, 1, 3, 2, 4).reshape(B * nq, H, qc, dh)
    pos = jnp.minimum(offset + jnp.arange(nq * qc), T - 1).reshape(nq, qc)
    posr = jnp.tile(pos, (B, 1))
    bidx = jnp.repeat(jnp.arange(B), nq)
    hh = jnp.arange(H)[:, None, None]
    offs = jnp.arange(MB_BLOCK)

    def one(args):
        qi, pi, b = args
        kbb, vbb, kmb = kb[b], vb[b], kmean[b]
        own = pi // MB_BLOCK
        bs = jnp.einsum('hqd,hnd->hqn', qi.astype(jnp.float32), kmb)
        past = jnp.arange(nb)[None, None, :] < own[None, :, None]
        bs = jnp.where(past, bs, -jnp.inf)
        _, top = lax.top_k(bs, k_sel)
        valid = top < own[None, :, None]
        idx = jnp.concatenate([top, jnp.broadcast_to(own[None, :, None], (H, qc, 1))], axis=-1)
        kg = kbb[hh, idx]
        vg = vbb[hh, idx]
        s = jnp.einsum('hqd,hqjkd->hqjk', qi, kg).astype(jnp.float32)
        kpos = idx[..., None] * MB_BLOCK + offs
        sel = jnp.concatenate([valid, jnp.ones((H, qc, 1), bool)], axis=-1)
        mask = sel[..., None] & (kpos <= pi[None, :, None, None])
        s = jnp.where(mask, s, -jnp.inf)
        pr = jax.nn.softmax(s.reshape(H, qc, -1), axis=-1).reshape(s.shape).astype(vg.dtype)
        return jnp.einsum('hqjk,hqjkd->qhd', pr, vg)

    out = lax.map(one, (qp, posr, bidx))
    return out.reshape(B, nq * qc, H, dh)[:, :Lq]


def hybrid_layer(x, p, gla_s, ml_c, ml_n, ml_m, ml_conv, k_past, v_past):
    B, L, _ = x.shape
    h = rms_norm(x, p['norm_g'])
    u = h @ p['w_in']
    (g_q, g_k, g_v, g_a, g_z, m_x, m_i, m_f, m_o, m_z,
     b_q, b_k, b_v, b_z, gate) = jnp.split(u, np.cumsum(IN_SPLITS)[:-1].tolist(), axis=-1)

    gq = g_q.reshape(B, L, GLA_HEADS, GLA_DK) * (GLA_DK ** -0.5)
    gk = g_k.reshape(B, L, GLA_HEADS, GLA_DK)
    gv = g_v.reshape(B, L, GLA_HEADS, GLA_DV)
    log_a = jax.nn.log_sigmoid((g_a @ p['gla_w_a2'] + p['gla_b_a']).astype(jnp.float32)) / GLA_TAU
    go, gla_new = gla_scan(gq, gk, gv, log_a.reshape(B, L, GLA_HEADS, GLA_DK), gla_s)
    go = rms_norm(go.astype(x.dtype), p['gla_norm']).reshape(B, L, GLA_WIDTH) * jax.nn.silu(g_z)
    y_gla = go @ p['gla_w_out']

    xp = jnp.concatenate([ml_conv.astype(m_x.dtype), m_x], axis=1)
    conv = sum((xp[:, j:j + L] * p['ml_conv_w'][j] for j in range(ML_CONV)), p['ml_conv_b'])
    conv_new = xp[:, -(ML_CONV - 1):]
    mc = jax.nn.silu(conv)
    mch = mc.reshape(B, L, ML_HEADS, ML_DH)
    mq = jnp.einsum('blhd,hde->blhe', mch, p['ml_w_q'])
    mk = jnp.einsum('blhd,hde->blhe', mch, p['ml_w_k']) * (ML_DH ** -0.5)
    mv = jnp.einsum('blhd,hde->blhe', m_x.reshape(B, L, ML_HEADS, ML_DH), p['ml_w_v'])
    ig = (m_i + p['ml_b_i']).astype(jnp.float32)
    lf = jax.nn.log_sigmoid((m_f + p['ml_b_f']).astype(jnp.float32))
    mh, c_new, n_new, m_new = mlstm_scan(mq, mk, mv, ig, lf, ml_c, ml_n, ml_m)
    mh = mh.astype(x.dtype) * jax.nn.sigmoid(m_o).reshape(B, L, ML_HEADS, ML_DH)
    mh = rms_norm(mh, p['ml_norm']).reshape(B, L, ML_WIDTH) + p['ml_skip'] * mc
    y_ml = (mh * jax.nn.silu(m_z)) @ p['ml_w_out']

    bq = rms_norm(b_q.reshape(B, L, MB_HEADS, MB_DH), p['mb_q_norm']) * (MB_DH ** -0.5)
    bk = rms_norm(b_k.reshape(B, L, MB_HEADS, MB_DH), p['mb_k_norm'])
    bv = b_v.reshape(B, L, MB_HEADS, MB_DH)
    k_all = jnp.concatenate([k_past.astype(bk.dtype), bk], axis=1)
    v_all = jnp.concatenate([v_past.astype(bv.dtype), bv], axis=1)
    bo = moba_attention(bq, k_all, v_all, k_past.shape[1])
    y_mb = (bo.reshape(B, L, MB_WIDTH) * jax.nn.silu(b_z)) @ p['mb_w_out']

    g = jax.nn.sigmoid(gate).reshape(B, L, N_BRANCH, D_MODEL)
    y = g[:, :, 0] * y_gla + g[:, :, 1] * y_ml + g[:, :, 2] * y_mb
    x = x + y @ p['w_out']
    dt = x.dtype
    return x, (gla_new.astype(dt), c_new.astype(dt), n_new.astype(dt), m_new.astype(dt),
               conv_new.astype(dt), bk, bv)


def setup_inputs(seed: int = 0) -> dict:
    key = jax.random.key(seed)
    ks = jax.random.split(key, 40)
    f32 = jnp.float32
    n_pages = PAST_LEN // PAGE_SIZE
    n_pool = int(math.ceil(POOL_FACTOR * DEC_BATCH * n_pages))

    def nrm(k, shape, scale):
        return jax.random.normal(k, shape, f32) * scale

    def gain(k, shape):
        return 1.0 + 0.02 * jax.random.normal(k, shape, f32)

    page_table = jax.random.permutation(ks[4], n_pool)[:DEC_BATCH * n_pages].reshape(DEC_BATCH, n_pages).astype(jnp.int32)
    return {
        "x_prompt": nrm(ks[0], (BATCH, SEQ, D_MODEL), 1.0),
        "x_sample": nrm(ks[1], (DEC_BATCH, DEC_SEQ, D_MODEL), 1.0),
        "cache_k": nrm(ks[2], (DEPTH, n_pool, PAGE_SIZE, MB_HEADS, MB_DH), 1.0),
        "cache_v": nrm(ks[3], (DEPTH, n_pool, PAGE_SIZE, MB_HEADS, MB_DH), 1.0),
        "page_table": page_table,
        "state_gla": nrm(ks[5], (DEPTH, DEC_BATCH, GLA_HEADS, GLA_DK, GLA_DV), 1.0),
        "state_mlstm_c": nrm(ks[6], (DEPTH, DEC_BATCH, ML_HEADS, ML_DH, ML_DH), 1.0),
        "state_mlstm_n": nrm(ks[7], (DEPTH, DEC_BATCH, ML_HEADS, ML_DH), 1.0),
        "state_mlstm_m": nrm(ks[8], (DEPTH, DEC_BATCH, ML_HEADS), 1.0),
        "state_mlstm_conv": nrm(ks[9], (DEPTH, DEC_BATCH, ML_CONV - 1, ML_WIDTH), 1.0),
        "norm_g": gain(ks[10], (DEPTH, D_MODEL)),
        "w_in": nrm(ks[11], (DEPTH, D_MODEL, D_IN), D_MODEL ** -0.5),
        "gla_w_a2": nrm(ks[12], (DEPTH, GLA_GATE_RANK, GLA_HEADS * GLA_DK), GLA_GATE_RANK ** -0.5),
        "gla_b_a": nrm(ks[13], (DEPTH, GLA_HEADS * GLA_DK), 0.1),
        "gla_norm": gain(ks[14], (DEPTH, GLA_DV)),
        "gla_w_out": nrm(ks[15], (DEPTH, GLA_WIDTH, D_MODEL), GLA_WIDTH ** -0.5),
        "ml_conv_w": nrm(ks[16], (DEPTH, ML_CONV, ML_WIDTH), ML_CONV ** -0.5),
        "ml_conv_b": nrm(ks[17], (DEPTH, ML_WIDTH), 0.02),
        "ml_w_q": nrm(ks[18], (DEPTH, ML_HEADS, ML_DH, ML_DH), ML_DH ** -0.5),
        "ml_w_k": nrm(ks[19], (DEPTH, ML_HEADS, ML_DH, ML_DH), ML_DH ** -0.5),
        "ml_w_v": nrm(ks[20], (DEPTH, ML_HEADS, ML_DH, ML_DH), ML_DH ** -0.5),
        "ml_b_i": nrm(ks[21], (DEPTH, ML_HEADS), 0.1),
        "ml_b_f": jnp.linspace(3.0, 6.0, ML_HEADS, dtype=f32)[None, :] + nrm(ks[22], (DEPTH, ML_HEADS), 0.1),
        "ml_norm": gain(ks[23], (DEPTH, ML_DH)),
        "ml_skip": gain(ks[24], (DEPTH, ML_WIDTH)),
        "ml_w_out": nrm(ks[25], (DEPTH, ML_WIDTH, D_MODEL), ML_WIDTH ** -0.5),
        "mb_q_norm": gain(ks[26], (DEPTH, MB_DH)),
        "mb_k_norm": gain(ks[27], (DEPTH, MB_DH)),
        "mb_w_out": nrm(ks[28], (DEPTH, MB_WIDTH, D_MODEL), MB_WIDTH ** -0.5),
        "w_out": nrm(ks[29], (DEPTH, D_MODEL, D_MODEL), D_MODEL ** -0.5),
    }


def reference(x_prompt, x_sample, cache_k, cache_v, page_table, state_gla, state_mlstm_c, state_mlstm_n,
              state_mlstm_m, state_mlstm_conv, norm_g, w_in, gla_w_a2, gla_b_a, gla_norm, gla_w_out,
              ml_conv_w, ml_conv_b, ml_w_q, ml_w_k, ml_w_v, ml_b_i, ml_b_f, ml_norm, ml_skip, ml_w_out,
              mb_q_norm, mb_k_norm, mb_w_out, w_out):
    B = x_prompt.shape[0]
    DB = x_sample.shape[0]
    n_pages = page_table.shape[1]
    dt = x_prompt.dtype
    xp, xs = x_prompt, x_sample
    p_list, s_list = [], []
    for l in range(DEPTH):
        p = dict(norm_g=norm_g[l], w_in=w_in[l], gla_w_a2=gla_w_a2[l], gla_b_a=gla_b_a[l],
                 gla_norm=gla_norm[l], gla_w_out=gla_w_out[l], ml_conv_w=ml_conv_w[l],
                 ml_conv_b=ml_conv_b[l], ml_w_q=ml_w_q[l], ml_w_k=ml_w_k[l], ml_w_v=ml_w_v[l],
                 ml_b_i=ml_b_i[l], ml_b_f=ml_b_f[l], ml_norm=ml_norm[l], ml_skip=ml_skip[l],
                 ml_w_out=ml_w_out[l], mb_q_norm=mb_q_norm[l], mb_k_norm=mb_k_norm[l],
                 mb_w_out=mb_w_out[l], w_out=w_out[l])
        xp, sp = hybrid_layer(
            xp, p,
            jnp.zeros((B, GLA_HEADS, GLA_DK, GLA_DV), dt),
            jnp.zeros((B, ML_HEADS, ML_DH, ML_DH), dt),
            jnp.zeros((B, ML_HEADS, ML_DH), dt),
            jnp.zeros((B, ML_HEADS), dt),
            jnp.zeros((B, ML_CONV - 1, ML_WIDTH), dt),
            jnp.zeros((B, 0, MB_HEADS, MB_DH), dt),
            jnp.zeros((B, 0, MB_HEADS, MB_DH), dt))
        p_list.append(sp)
        k_past = cache_k[l][page_table].reshape(DB, n_pages * PAGE_SIZE, MB_HEADS, MB_DH)
        v_past = cache_v[l][page_table].reshape(DB, n_pages * PAGE_SIZE, MB_HEADS, MB_DH)
        xs, ss = hybrid_layer(xs, p, state_gla[l], state_mlstm_c[l], state_mlstm_n[l], state_mlstm_m[l],
                              state_mlstm_conv[l], k_past, v_past)
        s_list.append(ss)
    p_gla, p_mc, p_mn, p_mm, p_conv, p_k, p_v = [jnp.stack(z) for z in zip(*p_list)]
    s_gla, s_mc, s_mn, s_mm, s_conv, s_k, s_v = [jnp.stack(z) for z in zip(*s_list)]
    return (xp, xs, p_gla, p_mc, p_mn, p_mm, p_conv, p_k, p_v,
            s_gla, s_mc, s_mn, s_mm, s_conv, s_k, s_v)
```

```python
import functools

import jax
import jax.numpy as jnp
from jax import lax
from jax.experimental import pallas as pl
from jax.experimental.pallas import tpu as pltpu

F32 = jnp.float32
BF16 = jnp.bfloat16
HI = lax.Precision.HIGHEST

D_MODEL = 1024
GLA_HEADS, GLA_DK, GLA_DV, GLA_RANK, GLA_TAU, GLA_CHUNK = 4, 64, 128, 16, 16.0, 64
GLA_KW = GLA_HEADS * GLA_DK
GLA_W = GLA_HEADS * GLA_DV
ML_HEADS, ML_DH, ML_CONV, ML_CHUNK = 4, 128, 4, 64
ML_W = ML_HEADS * ML_DH
MB_HEADS, MB_DH, MB_BLOCK, MB_TOPK = 8, 64, 256, 3
MB_W = MB_HEADS * MB_DH
EPS = 1e-6
NEG = -1e30

_OFF = {}
_o = 0
for _n, _w in (("g_q", GLA_KW), ("g_k", GLA_KW), ("g_v", GLA_W), ("g_a", GLA_RANK), ("g_z", GLA_W),
               ("m_x", ML_W), ("m_i", ML_HEADS), ("m_f", ML_HEADS), ("m_o", ML_W), ("m_z", ML_W),
               ("b_q", MB_W), ("b_k", MB_W), ("b_v", MB_W), ("b_z", MB_W), ("gate", 3 * D_MODEL)):
    _OFF[_n] = (_o, _w)
    _o += _w
_BIG_ORDER = ("g_q", "g_k", "g_v", "g_z", "m_x", "m_o", "m_z", "b_q", "b_k", "b_v", "b_z", "gate")
_BIG_OFF = {}
_o = 0
for _n in _BIG_ORDER:
    _BIG_OFF[_n] = _o
    _o += _OFF[_n][1]
D_BIG = _o
SMALL_W = 128
SM_A, SM_I, SM_F = 0, GLA_RANK, GLA_RANK + ML_HEADS

VMEM_LIMIT = 56 * 1024 * 1024


def _dot(a, b, **kw):
    return jnp.dot(a, b, preferred_element_type=F32, **kw)


def _dot_nt(a, b, **kw):
    return lax.dot_general(a, b, (((1,), (1,)), ((), ())), preferred_element_type=F32, **kw)


def _dot_tn(a, b, **kw):
    return lax.dot_general(a, b, (((0,), (0,)), ((), ())), preferred_element_type=F32, **kw)


def _mx(x, bf):
    return x.astype(BF16 if bf else F32)


def _sigmoid(x):
    return 1.0 / (1.0 + jnp.exp(-x))


def _silu(x):
    return x * _sigmoid(x)


def _log_sigmoid(x):
    return jnp.minimum(x, 0.0) - jnp.log(1.0 + jnp.exp(-jnp.abs(x)))


def _params(sem):
    return pltpu.CompilerParams(dimension_semantics=sem, vmem_limit_bytes=VMEM_LIMIT)


def _in_proj_kernel(x_ref, g_ref, wb_ref, ws_ref, wst_ref, ub_ref, us_ref, ust_ref, h_ref):
    @pl.when(pl.program_id(1) == 0)
    def _():
        x = x_ref[...]
        h = x * lax.rsqrt(jnp.mean(x * x, axis=-1, keepdims=True) + EPS) * g_ref[...]
        hb = h.astype(BF16)
        h_ref[...] = hb
        us_ref[...] = _dot(hb, ws_ref[...])
        ust_ref[0] = _dot_nt(wst_ref[...], hb)

    ub_ref[...] = _dot(h_ref[...], wb_ref[...])


def _in_proj(x2, g, wb, ws, wst):
    n = x2.shape[0]
    tm = min(1024, n)
    tn = 1024
    return pl.pallas_call(
        _in_proj_kernel,
        grid=(n // tm, D_BIG // tn),
        in_specs=[
            pl.BlockSpec((tm, D_MODEL), lambda i, j: (i, 0)),
            pl.BlockSpec((1, D_MODEL), lambda i, j: (0, 0)),
            pl.BlockSpec((D_MODEL, tn), lambda i, j: (0, j)),
            pl.BlockSpec((D_MODEL, SMALL_W), lambda i, j: (0, 0)),
            pl.BlockSpec((8, D_MODEL), lambda i, j: (0, 0)),
        ],
        out_specs=[
            pl.BlockSpec((tm, tn), lambda i, j: (i, j)),
            pl.BlockSpec((tm, SMALL_W), lambda i, j: (i, 0)),
            pl.BlockSpec((1, 8, tm), lambda i, j: (i, 0, 0)),
        ],
        out_shape=[
            jax.ShapeDtypeStruct((n, D_BIG), F32),
            jax.ShapeDtypeStruct((n, SMALL_W), F32),
            jax.ShapeDtypeStruct((n // tm, 8, tm), F32),
        ],
        scratch_shapes=[pltpu.VMEM((tm, D_MODEL), BF16)],
        compiler_params=_params(("parallel", "arbitrary")),
        name="in_proj",
    )(x2, g, wb, ws, wst)


def _gla_kernel(q_ref, k_ref, v_ref, z_ref, sm_ref, wa_ref, ba_ref, gn_ref, s0_ref,
                o_ref, s_out_ref, st_ref, *, C, nC, nT):
    t = pl.program_id(1)

    @pl.when(t == 0)
    def _():
        st_ref[...] = s0_ref[0]

    bf = C % 16 == 0
    xa = _dot(_mx(sm_ref[...], bf), _mx(wa_ref[...], bf)) + ba_ref[...]
    log_a = _log_sigmoid(xa) * (1.0 / GLA_TAU)

    lane = lax.broadcasted_iota(jnp.int32, (C, GLA_KW), 1)
    lane_s = lax.broadcasted_iota(jnp.int32, (GLA_DV, GLA_KW), 1)
    row_i = lax.broadcasted_iota(jnp.int32, (C, C), 0)
    col_i = lax.broadcasted_iota(jnp.int32, (C, C), 1)
    tri = row_i >= col_i
    trif = jnp.where(tri, 1.0, 0.0)
    mid = C // 2 - 1
    gn = gn_ref[...]

    for c in range(nC):
        sl = slice(c * C, (c + 1) * C)
        b = _dot(trif, log_a[sl], precision=HI)
        r = b[mid:mid + 1]
        bl = b[C - 1:C]
        qd = q_ref[sl, :] * (GLA_DK ** -0.5) * jnp.exp(b - r)
        kd = k_ref[sl, :] * jnp.exp(r - b)
        qb = qd * jnp.exp(r)
        kdec = kd * jnp.exp(bl - r)
        st = st_ref[...]
        st_m = _mx(st, bf)
        kd_m = _mx(kd, bf)
        v_c = v_ref[sl, :]
        for h in range(GLA_HEADS):
            hm = (lane >= h * GLA_DK) & (lane < (h + 1) * GLA_DK)
            a = _dot_nt(_mx(jnp.where(hm, qd, 0.0), bf), kd_m)
            a = jnp.where(tri, a, 0.0)
            v_h = v_c[:, h * GLA_DV:(h + 1) * GLA_DV]
            o_h = _dot(_mx(a, bf), _mx(v_h, bf)) + _dot_nt(_mx(jnp.where(hm, qb, 0.0), bf), st_m)
            y = o_h * lax.rsqrt(jnp.mean(o_h * o_h, axis=-1, keepdims=True) + EPS) * gn
            o_ref[sl, h * GLA_DV:(h + 1) * GLA_DV] = y * _silu(z_ref[sl, h * GLA_DV:(h + 1) * GLA_DV])
        full = _dot_tn(_mx(v_c, bf), _mx(kdec, bf))
        upd = jnp.zeros((GLA_DV, GLA_KW), F32)
        for h in range(GLA_HEADS):
            hm_s = (lane_s >= h * GLA_DK) & (lane_s < (h + 1) * GLA_DK)
            upd = upd + jnp.where(hm_s, full[h * GLA_DV:(h + 1) * GLA_DV], 0.0)
        st_ref[...] = jnp.exp(bl) * st + upd

    @pl.when(t == nT - 1)
    def _():
        s_out_ref[0] = st_ref[...]


def _gla(ub, us, wa, ba, gn, s0t, B, L, T):
    nT = L // T
    C = min(GLA_CHUNK, T)
    nC = T // C
    row = lambda b, t: b * nT + t
    kern = functools.partial(_gla_kernel, C=C, nC=nC, nT=nT)
    return pl.pallas_call(
        kern,
        grid=(B, nT),
        in_specs=[
            pl.BlockSpec((T, GLA_KW), lambda b, t: (row(b, t), _BIG_OFF["g_q"] // GLA_KW)),
            pl.BlockSpec((T, GLA_KW), lambda b, t: (row(b, t), _BIG_OFF["g_k"] // GLA_KW)),
            pl.BlockSpec((T, GLA_W), lambda b, t: (row(b, t), _BIG_OFF["g_v"] // GLA_W)),
            pl.BlockSpec((T, GLA_W), lambda b, t: (row(b, t), _BIG_OFF["g_z"] // GLA_W)),
            pl.BlockSpec((T, SMALL_W), lambda b, t: (row(b, t), 0)),
            pl.BlockSpec((SMALL_W, GLA_KW), lambda b, t: (0, 0)),
            pl.BlockSpec((1, GLA_KW), lambda b, t: (0, 0)),
            pl.BlockSpec((1, GLA_DV), lambda b, t: (0, 0)),
            pl.BlockSpec((1, GLA_DV, GLA_KW), lambda b, t: (b, 0, 0)),
        ],
        out_specs=[
            pl.BlockSpec((T, GLA_W), lambda b, t: (row(b, t), 0)),
            pl.BlockSpec((1, GLA_DV, GLA_KW), lambda b, t: (b, 0, 0)),
        ],
        out_shape=[
            jax.ShapeDtypeStruct((B * L, GLA_W), F32),
            jax.ShapeDtypeStruct((B, GLA_DV, GLA_KW), F32),
        ],
        scratch_shapes=[pltpu.VMEM((GLA_DV, GLA_KW), F32)],
        compiler_params=_params(("parallel", "arbitrary")),
        name="gla",
    )(ub, ub, ub, ub, us, wa, ba, gn, s0t)


def _mlstm_kernel(x_ref, og_ref, z_ref, sm_ref, smt_ref, cw_ref, cb_ref, wq_ref, wk_ref, wv_ref,
                  brow_ref, bcol_ref, nrm_ref, skip_ref, c0_ref, n0_ref, m0_ref, cv0_ref,
                  out_ref, c_out, n_out, m_out, cv_out,
                  xbuf, c_s, n_s, m_s, *, T, C, nC, nT):
    t = pl.program_id(1)

    @pl.when(t == 0)
    def _():
        xbuf[0:8, :] = cv0_ref[0]
        c_s[...] = c0_ref[0]
        n_s[...] = n0_ref[0]
        m_s[...] = m0_ref[0]

    x = x_ref[...]
    xbuf[8:8 + T, :] = x
    conv = cb_ref[...]
    for j in range(ML_CONV):
        conv = conv + xbuf[5 + j:5 + j + T, :] * cw_ref[j:j + 1, :]
    xbuf[0:8, :] = xbuf[T:T + 8, :]
    mc = _silu(conv)

    smc = sm_ref[...] + brow_ref[...]
    lsc = _log_sigmoid(smc)
    smt = smt_ref[0] + bcol_ref[...]
    rsel = lax.broadcasted_iota(jnp.int32, (8, T), 0) < ML_HEADS
    smt = jnp.where(rsel, smt, _log_sigmoid(smt))

    row_i = lax.broadcasted_iota(jnp.int32, (C, C), 0)
    col_i = lax.broadcasted_iota(jnp.int32, (C, C), 1)
    tri = row_i >= col_i
    trif = jnp.where(tri, 1.0, 0.0)
    triuf = jnp.where(row_i <= col_i, 1.0, 0.0)

    bf = C % 16 == 0
    qs, ks, vs = [], [], []
    for h in range(ML_HEADS):
        hs = slice(h * ML_DH, (h + 1) * ML_DH)
        mc_h = _mx(mc[:, hs], bf)
        qs.append(_dot(mc_h, _mx(wq_ref[h], bf)))
        ks.append(_dot(mc_h, _mx(wk_ref[h], bf)) * (ML_DH ** -0.5))
        vs.append(_dot(_mx(x[:, hs], bf), _mx(wv_ref[h], bf)))

    for c in range(nC):
        sl = slice(c * C, (c + 1) * C)
        f_cols = _dot(trif, lsc[sl], precision=HI)
        f_rows = _dot(smt[:, sl], triuf, precision=HI)
        for h in range(ML_HEADS):
            hs = slice(h * ML_DH, (h + 1) * ML_DH)
            fc = f_cols[:, SM_F + h:SM_F + h + 1]
            ic = smc[sl, SM_I + h:SM_I + h + 1]
            fr = f_rows[ML_HEADS + h:ML_HEADS + h + 1, :]
            ir = smt[h:h + 1, sl]
            m_prev = m_s[h][:, 0:1]
            q_h, k_h, v_h = qs[h][sl], ks[h][sl], vs[h][sl]
            c_h = c_s[h]
            n_h = n_s[h]
            dm = jnp.where(tri, fc - fr + ir, NEG)
            lin = fc + m_prev
            m_t = jnp.maximum(lin, jnp.max(dm, axis=1, keepdims=True))
            w = jnp.exp(dm - m_t)
            w_in = jnp.exp(lin - m_t)
            q_m = _mx(q_h, bf)
            k_m = _mx(k_h, bf)
            sqk = _dot_nt(q_m, k_m) * w
            num = _dot(_mx(sqk, bf), _mx(v_h, bf)) + w_in * _dot_nt(q_m, _mx(c_h, bf))
            den = jnp.sum(sqk, axis=1, keepdims=True) + w_in * jnp.sum(q_h * n_h, axis=1, keepdims=True)
            hh = num / jnp.maximum(jnp.abs(den), jnp.exp(-m_t))
            m_new = m_t[C - 1:C]
            f_last = fc[C - 1:C]
            w_end = jnp.exp(f_last - fc + ic - m_new)
            decay = jnp.exp(f_last + m_prev - m_new)
            c_s[h] = decay * c_h + _dot_tn(_mx(w_end * v_h, bf), k_m)
            n_s[h] = decay * n_h + jnp.sum(w_end * k_h, axis=0, keepdims=True)
            m_s[h] = jnp.broadcast_to(m_new, (1, ML_DH))
            mh = hh * _sigmoid(og_ref[sl, hs])
            mh = mh * lax.rsqrt(jnp.mean(mh * mh, axis=-1, keepdims=True) + EPS) * nrm_ref[...]
            mh = mh + skip_ref[:, hs] * mc[sl, hs]
            out_ref[sl, hs] = mh * _silu(z_ref[sl, hs])

    @pl.when(t == nT - 1)
    def _():
        c_out[0] = c_s[...]
        n_out[0] = n_s[...]
        m_out[0] = m_s[...]
        cv_out[0] = xbuf[0:8, :]


def _mlstm(ub, us, ust, p, c0, n0, m0, cv0, B, L, T):
    nT = L // T
    C = min(ML_CHUNK, T)
    nC = T // C
    row = lambda b, t: b * nT + t
    kern = functools.partial(_mlstm_kernel, T=T, C=C, nC=nC, nT=nT)
    full2 = lambda b, t: (0, 0)
    full3 = lambda b, t: (0, 0, 0)
    st4 = lambda b, t: (b, 0, 0, 0)
    return pl.pallas_call(
        kern,
        grid=(B, nT),
        in_specs=[
            pl.BlockSpec((T, ML_W), lambda b, t: (row(b, t), _BIG_OFF["m_x"] // ML_W)),
            pl.BlockSpec((T, ML_W), lambda b, t: (row(b, t), _BIG_OFF["m_o"] // ML_W)),
            pl.BlockSpec((T, ML_W), lambda b, t: (row(b, t), _BIG_OFF["m_z"] // ML_W)),
            pl.BlockSpec((T, SMALL_W), lambda b, t: (row(b, t), 0)),
            pl.BlockSpec((1, 8, T), lambda b, t: (row(b, t), 0, 0)),
            pl.BlockSpec((ML_CONV, ML_W), full2),
            pl.BlockSpec((1, ML_W), full2),
            pl.BlockSpec((ML_HEADS, ML_DH, ML_DH), full3),
            pl.BlockSpec((ML_HEADS, ML_DH, ML_DH), full3),
            pl.BlockSpec((ML_HEADS, ML_DH, ML_DH), full3),
            pl.BlockSpec((1, SMALL_W), full2),
            pl.BlockSpec((8, 1), full2),
            pl.BlockSpec((1, ML_DH), full2),
            pl.BlockSpec((1, ML_W), full2),
            pl.BlockSpec((1, ML_HEADS, ML_DH, ML_DH), st4),
            pl.BlockSpec((1, ML_HEADS, 1, ML_DH), st4),
            pl.BlockSpec((1, ML_HEADS, 1, ML_DH), st4),
            pl.BlockSpec((1, 8, ML_W), lambda b, t: (b, 0, 0)),
        ],
        out_specs=[
            pl.BlockSpec((T, ML_W), lambda b, t: (row(b, t), 0)),
            pl.BlockSpec((1, ML_HEADS, ML_DH, ML_DH), st4),
            pl.BlockSpec((1, ML_HEADS, 1, ML_DH), st4),
            pl.BlockSpec((1, ML_HEADS, 1, ML_DH), st4),
            pl.BlockSpec((1, 8, ML_W), lambda b, t: (b, 0, 0)),
        ],
        out_shape=[
            jax.ShapeDtypeStruct((B * L, ML_W), F32),
            jax.ShapeDtypeStruct((B, ML_HEADS, ML_DH, ML_DH), F32),
            jax.ShapeDtypeStruct((B, ML_HEADS, 1, ML_DH), F32),
            jax.ShapeDtypeStruct((B, ML_HEADS, 1, ML_DH), F32),
            jax.ShapeDtypeStruct((B, 8, ML_W), F32),
        ],
        scratch_shapes=[
            pltpu.VMEM((T + 8, ML_W), F32),
            pltpu.VMEM((ML_HEADS, ML_DH, ML_DH), F32),
            pltpu.VMEM((ML_HEADS, 1, ML_DH), F32),
            pltpu.VMEM((ML_HEADS, 1, ML_DH), F32),
        ],
        compiler_params=_params(("parallel", "arbitrary")),
        name="mlstm",
    )(ub, ub, ub, us, ust, p["cw"], p["cb"], p["wq"], p["wk"], p["wv"], p["brow"], p["bcol"],
      p["ml_norm"], p["ml_skip"], c0, n0, m0, cv0)


def _qknorm_kernel(q_ref, k_ref, gq_ref, gk_ref, gm_ref, qn_ref, kn_ref, km_ref, *, T):
    gm = gm_ref[...]

    def nrm(x, g):
        s = x * x
        hi = s.astype(BF16)
        lo = (s - hi.astype(F32)).astype(BF16)
        ms = _dot(hi, gm) + _dot(lo, gm)
        return x * lax.rsqrt(ms + EPS) * g

    qn_ref[...] = nrm(q_ref[...], gq_ref[...]) * (MB_DH ** -0.5)
    kn = nrm(k_ref[...], gk_ref[...])
    kn_ref[...] = kn
    km_ref[0] = jnp.sum(kn, axis=0, keepdims=True) * (1.0 / T)


def _qknorm(ub, gq, gk, gmat, n, T):
    kern = functools.partial(_qknorm_kernel, T=T)
    return pl.pallas_call(
        kern,
        grid=(n // T,),
        in_specs=[
            pl.BlockSpec((T, MB_W), lambda i: (i, _BIG_OFF["b_q"] // MB_W)),
            pl.BlockSpec((T, MB_W), lambda i: (i, _BIG_OFF["b_k"] // MB_W)),
            pl.BlockSpec((1, MB_W), lambda i: (0, 0)),
            pl.BlockSpec((1, MB_W), lambda i: (0, 0)),
            pl.BlockSpec((MB_W, MB_W), lambda i: (0, 0)),
        ],
        out_specs=[
            pl.BlockSpec((T, MB_W), lambda i: (i, 0)),
            pl.BlockSpec((T, MB_W), lambda i: (i, 0)),
            pl.BlockSpec((1, 1, MB_W), lambda i: (i, 0, 0)),
        ],
        out_shape=[
            jax.ShapeDtypeStruct((n, MB_W), F32),
            jax.ShapeDtypeStruct((n, MB_W), F32),
            jax.ShapeDtypeStruct((n // T, 1, MB_W), F32),
        ],
        compiler_params=_params(("parallel",)),
        name="qknorm",
    )(ub, ub, gq, gk, gmat)


def _top_mask(bs, allowed, lane_b, n_blocks):
    bsm = jnp.where(allowed, bs, -jnp.inf)
    rank = jnp.zeros(bs.shape, jnp.int32)
    for n in range(n_blocks):
        col = bsm[:, n:n + 1]
        beats = (col > bsm) | ((col == bsm) & (lane_b > n))
        rank = rank + jnp.where(beats, 1, 0)
    return jnp.where(allowed & (rank < MB_TOPK), 1.0, 0.0)


def _moba_prompt_kernel(q_ref, k_ref, v_ref, km_ref, z_ref, o_ref, acc_ref, m_ref, l_ref, *, nB):
    i = pl.program_id(2)
    Q = MB_BLOCK
    q = q_ref[...]
    lane = lax.broadcasted_iota(jnp.int32, (Q, 2 * MB_DH), 1)
    h0 = lane < MB_DH
    qm = (jnp.where(h0, q, 0.0), jnp.where(h0, 0.0, q))
    qb = tuple(x.astype(BF16) for x in qm)
    km = km_ref[0]
    lane_b = lax.broadcasted_iota(jnp.int32, (Q, nB), 1)
    sels = tuple(_top_mask(_dot_nt(qm[h], km, precision=HI), lane_b < i, lane_b, nB) for h in range(2))

    row_i = lax.broadcasted_iota(jnp.int32, (Q, Q), 0)
    col_i = lax.broadcasted_iota(jnp.int32, (Q, Q), 1)
    causal = row_i >= col_i

    def block(j, keep_fn, first):
        start = pl.multiple_of(j * Q, Q)
        kj = k_ref[pl.ds(start, Q), :].astype(BF16)
        vj = v_ref[pl.ds(start, Q), :].astype(BF16)
        m_old = m_ref[...]
        rmax, rsum, pvs = [], [], []
        for h in range(2):
            s = jnp.where(keep_fn(h), _dot_nt(qb[h], kj), NEG)
            mx = jnp.max(s, axis=1, keepdims=True)
            if not first:
                mx = jnp.maximum(mx, m_old[:, h * MB_DH:h * MB_DH + 1])
            p = jnp.exp(s - mx)
            rmax.append(mx)
            rsum.append(jnp.sum(p, axis=1, keepdims=True))
            pvs.append(_dot(p.astype(BF16), vj))
        m_new = jnp.where(h0, rmax[0], rmax[1])
        l_add = jnp.where(h0, rsum[0], rsum[1])
        pv = jnp.where(h0, pvs[0], pvs[1])
        if first:
            m_ref[...] = m_new
            l_ref[...] = l_add
            acc_ref[...] = pv
        else:
            alpha = jnp.exp(m_old - m_new)
            m_ref[...] = m_new
            l_ref[...] = alpha * l_ref[...] + l_add
            acc_ref[...] = alpha * acc_ref[...] + pv

    block(i, lambda h: causal, True)

    def body(j, carry):
        def keep(h):
            return jnp.sum(jnp.where(lane_b == j, sels[h], 0.0), axis=1, keepdims=True) > 0.0
        block(j, keep, False)
        return carry

    lax.fori_loop(0, i, body, 0)
    o_ref[...] = acc_ref[...] / l_ref[...] * _silu(z_ref[...])


def _moba_prompt(qn, kn, ub, km, B, L):
    Q = MB_BLOCK
    nB = L // Q
    W = 2 * MB_DH
    kern = functools.partial(_moba_prompt_kernel, nB=nB)
    return pl.pallas_call(
        kern,
        grid=(B, MB_W // W, nB),
        in_specs=[
            pl.BlockSpec((Q, W), lambda b, p, i: (b * nB + i, p)),
            pl.BlockSpec((L, W), lambda b, p, i: (b, p)),
            pl.BlockSpec((L, W), lambda b, p, i: (b, _BIG_OFF["b_v"] // W + p)),
            pl.BlockSpec((1, nB, W), lambda b, p, i: (b, 0, p)),
            pl.BlockSpec((Q, W), lambda b, p, i: (b * nB + i, _BIG_OFF["b_z"] // W + p)),
        ],
        out_specs=pl.BlockSpec((Q, W), lambda b, p, i: (b * nB + i, p)),
        out_shape=jax.ShapeDtypeStruct((B * L, MB_W), F32),
        scratch_shapes=[pltpu.VMEM((Q, W), F32)] * 3,
        compiler_params=_params(("parallel", "parallel", "arbitrary")),
        name="moba_prompt",
    )(qn, kn, ub, km, ub)


def _moba_sample_kernel(pt_ref, q_ref, kn_ref, vn_ref, z_ref, *refs, PG, NJ, LQ):
    kp = refs[:PG]
    vp = refs[PG:2 * PG]
    o_ref = refs[2 * PG]
    km_ref, mp_ref, lp_ref, op_ref = refs[2 * PG + 1:]
    del pt_ref
    j = pl.program_id(1)
    nbs = PG // 2
    nblk = nbs * NJ
    R = MB_HEADS * LQ
    q = q_ref[...]
    qt = jnp.concatenate([q] * MB_HEADS, axis=0)
    lq_bits = LQ.bit_length() - 1
    r_i = lax.broadcasted_iota(jnp.int32, (R, MB_W), 0)
    l_i = lax.broadcasted_iota(jnp.int32, (R, MB_W), 1)
    hmask = lax.shift_right_logical(l_i, 6) == lax.shift_right_logical(r_i, lq_bits)
    qe = jnp.where(hmask, qt, 0.0)
    qeb = qe.astype(BF16)
    lane128 = lax.broadcasted_iota(jnp.int32, (R, 128), 1)

    @pl.when(j == 0)
    def _():
        mp_ref[...] = jnp.zeros_like(mp_ref)
        lp_ref[...] = jnp.zeros_like(lp_ref)

    kms = []
    for c in range(nbs):
        k0 = kp[2 * c][...]
        k1 = kp[2 * c + 1][...]
        kms.append((jnp.sum(k0, axis=0, keepdims=True) + jnp.sum(k1, axis=0, keepdims=True)) * (1.0 / MB_BLOCK))
        s = jnp.concatenate([_dot_nt(qeb, k0.astype(BF16)), _dot_nt(qeb, k1.astype(BF16))], axis=1)
        mx = jnp.max(s, axis=1, keepdims=True)
        p = jnp.exp(s - mx)
        ls = jnp.sum(p, axis=1, keepdims=True)
        pb = p.astype(BF16)
        o_c = _dot(pb[:, :128], vp[2 * c][...].astype(BF16)) + _dot(pb[:, 128:], vp[2 * c + 1][...].astype(BF16))
        n = j * nbs + c
        mp_ref[...] = jnp.where(lane128 == n, mx, mp_ref[...])
        lp_ref[...] = jnp.where(lane128 == n, ls, lp_ref[...])
        op_ref[n] = o_c
    km_ref[pl.ds(pl.multiple_of(j * nbs, nbs), nbs), :] = jnp.concatenate(kms, axis=0)

    @pl.when(j == NJ - 1)
    def _():
        bs = _dot_nt(qe, km_ref[...], precision=HI)
        lane_b = lax.broadcasted_iota(jnp.int32, (R, nblk), 1)
        sel = _top_mask(bs, lane_b >= 0, lane_b, nblk) > 0.0
        mp = mp_ref[:, :nblk]
        lp = lp_ref[:, :nblk]
        s_own = _dot_nt(qe, kn_ref[...])
        r_o = lax.broadcasted_iota(jnp.int32, (R, LQ), 0)
        c_o = lax.broadcasted_iota(jnp.int32, (R, LQ), 1)
        s_own = jnp.where(c_o <= (r_o & (LQ - 1)), s_own, NEG)
        m_all = jnp.maximum(jnp.max(s_own, axis=1, keepdims=True),
                            jnp.max(jnp.where(sel, mp, NEG), axis=1, keepdims=True))
        p_own = jnp.exp(s_own - m_all)
        w = jnp.where(sel, jnp.exp(mp - m_all), 0.0)
        l_all = jnp.sum(p_own, axis=1, keepdims=True) + jnp.sum(w * lp, axis=1, keepdims=True)
        o_all = _dot(p_own, vn_ref[...])
        for n in range(nblk):
            o_all = o_all + w[:, n:n + 1] * op_ref[n]
        o_all = o_all / l_all
        lane_o = lax.broadcasted_iota(jnp.int32, (LQ, MB_W), 1)
        out = jnp.zeros((LQ, MB_W), F32)
        for h in range(MB_HEADS):
            hm = (lane_o >= h * MB_DH) & (lane_o < (h + 1) * MB_DH)
            out = out + jnp.where(hm, o_all[h * LQ:(h + 1) * LQ], 0.0)
        o_ref[...] = out * _silu(z_ref[...])


def _moba_sample(qn, kn, ub, cache_k, cache_v, page_table, layer, DB, LQ):
    n_pages = page_table.shape[1]
    PG = min(16, n_pages)
    NJ = n_pages // PG
    nblk = n_pages // 2
    R = MB_HEADS * LQ
    page = cache_k.shape[2]
    ck = cache_k.reshape(cache_k.shape[0], cache_k.shape[1], page, MB_W)
    cv = cache_v.reshape(cache_v.shape[0], cache_v.shape[1], page, MB_W)

    def page_spec(g):
        return pl.BlockSpec((None, None, page, MB_W), lambda b, j, pt: (layer, pt[b, j * PG + g], 0, 0))

    kern = functools.partial(_moba_sample_kernel, PG=PG, NJ=NJ, LQ=LQ)
    grid_spec = pltpu.PrefetchScalarGridSpec(
        num_scalar_prefetch=1,
        grid=(DB, NJ),
        in_specs=[
            pl.BlockSpec((LQ, MB_W), lambda b, j, pt: (b, 0)),
            pl.BlockSpec((LQ, MB_W), lambda b, j, pt: (b, 0)),
            pl.BlockSpec((LQ, MB_W), lambda b, j, pt: (b, _BIG_OFF["b_v"] // MB_W)),
            pl.BlockSpec((LQ, MB_W), lambda b, j, pt: (b, _BIG_OFF["b_z"] // MB_W)),
        ] + [page_spec(g) for g in range(PG)] * 2,
        out_specs=pl.BlockSpec((LQ, MB_W), lambda b, j, pt: (b, 0)),
        scratch_shapes=[
            pltpu.VMEM((nblk, MB_W), F32),
            pltpu.VMEM((R, 128), F32),
            pltpu.VMEM((R, 128), F32),
            pltpu.VMEM((nblk, R, MB_W), F32),
        ],
    )
    return pl.pallas_call(
        kern,
        grid_spec=grid_spec,
        out_shape=jax.ShapeDtypeStruct((DB * LQ, MB_W), F32),
        compiler_params=_params(("parallel", "arbitrary")),
        name="moba_sample",
    )(page_table, qn, kn, ub, ub, *([ck] * PG), *([cv] * PG))


def _merge_kernel(x_ref, a_ref, m_ref, b_ref, g0_ref, g1_ref, g2_ref, wa_ref, wm_ref, wb_ref, wo_ref, o_ref):
    y = (_sigmoid(g0_ref[...]) * _dot(a_ref[...].astype(BF16), wa_ref[...])
         + _sigmoid(g1_ref[...]) * _dot(m_ref[...].astype(BF16), wm_ref[...])
         + _sigmoid(g2_ref[...]) * _dot(b_ref[...].astype(BF16), wb_ref[...]))
    o_ref[...] = x_ref[...] + _dot(y.astype(BF16), wo_ref[...])


def _merge(x2, ag, mg, bg, ub, wa, wm, wb, wo):
    n = x2.shape[0]
    T = min(512, n)
    gate0 = _BIG_OFF["gate"] // D_MODEL
    br = lambda i: (i, 0)
    wfull = lambda i: (0, 0)
    return pl.pallas_call(
        _merge_kernel,
        grid=(n // T,),
        in_specs=[
            pl.BlockSpec((T, D_MODEL), br),
            pl.BlockSpec((T, GLA_W), br),
            pl.BlockSpec((T, ML_W), br),
            pl.BlockSpec((T, MB_W), br),
            pl.BlockSpec((T, D_MODEL), lambda i: (i, gate0)),
            pl.BlockSpec((T, D_MODEL), lambda i: (i, gate0 + 1)),
            pl.BlockSpec((T, D_MODEL), lambda i: (i, gate0 + 2)),
            pl.BlockSpec((GLA_W, D_MODEL), wfull),
            pl.BlockSpec((ML_W, D_MODEL), wfull),
            pl.BlockSpec((MB_W, D_MODEL), wfull),
            pl.BlockSpec((D_MODEL, D_MODEL), wfull),
        ],
        out_specs=pl.BlockSpec((T, D_MODEL), br),
        out_shape=jax.ShapeDtypeStruct((n, D_MODEL), F32),
        compiler_params=_params(("parallel",)),
        name="merge",
    )(x2, ag, mg, bg, ub, ub, ub, wa, wm, wb, wo)


def _prep_layer(w):
    w_in = w["w_in"]

    def cols(name):
        o, n = _OFF[name]
        return w_in[:, o:o + n]

    wb = jnp.concatenate([cols(n) for n in _BIG_ORDER], axis=1).astype(BF16)
    small = jnp.concatenate([cols("g_a"), cols("m_i"), cols("m_f")], axis=1)
    ws = jnp.pad(small, ((0, 0), (0, SMALL_W - small.shape[1]))).astype(BF16)
    wst = jnp.concatenate([cols("m_i"), cols("m_f")], axis=1).T.astype(BF16)
    wa = jnp.pad(w["gla_w_a2"], ((SM_A, SMALL_W - SM_A - GLA_RANK), (0, 0))).astype(BF16)
    brow = jnp.zeros((1, SMALL_W), F32)
    brow = brow.at[0, SM_I:SM_I + ML_HEADS].set(w["ml_b_i"]).at[0, SM_F:SM_F + ML_HEADS].set(w["ml_b_f"])
    bcol = jnp.concatenate([w["ml_b_i"], w["ml_b_f"]]).reshape(8, 1)
    head = jnp.arange(MB_W) // MB_DH
    gmat = jnp.where(head[:, None] == head[None, :], 1.0 / MB_DH, 0.0).astype(BF16)
    return dict(
        norm_g=w["norm_g"].reshape(1, D_MODEL), wb=wb, ws=ws, wst=wst, wa=wa,
        ba=w["gla_b_a"].reshape(1, GLA_KW), gla_norm=w["gla_norm"].reshape(1, GLA_DV),
        gla_w_out=w["gla_w_out"].astype(BF16),
        cw=w["ml_conv_w"], cb=w["ml_conv_b"].reshape(1, ML_W),
        wq=w["ml_w_q"].astype(BF16), wk=w["ml_w_k"].astype(BF16), wv=w["ml_w_v"].astype(BF16),
        brow=brow, bcol=bcol, ml_norm=w["ml_norm"].reshape(1, ML_DH), ml_skip=w["ml_skip"].reshape(1, ML_W),
        ml_w_out=w["ml_w_out"].astype(BF16),
        gq=jnp.tile(w["mb_q_norm"], MB_HEADS).reshape(1, MB_W), gk=jnp.tile(w["mb_k_norm"], MB_HEADS).reshape(1, MB_W),
        gmat=gmat, mb_w_out=w["mb_w_out"].astype(BF16), w_out=w["w_out"].astype(BF16),
    )


def _layer(x, p, gla_s, ml_c, ml_n, ml_m, ml_conv, past):
    B, L, _ = x.shape
    n = B * L
    assert L % 8 == 0 and L >= ML_CONV - 1
    T = min(256, L)
    assert L % T == 0
    x2 = x.reshape(n, D_MODEL)
    ub, us, ust = _in_proj(x2, p["norm_g"], p["wb"], p["ws"], p["wst"])
    ust = ust.transpose(1, 0, 2).reshape(8, n // T, T).transpose(1, 0, 2)

    s0t = gla_s.reshape(B, GLA_KW, GLA_DV).transpose(0, 2, 1)
    ag, s_new_t = _gla(ub, us, p["wa"], p["ba"], p["gla_norm"], s0t, B, L, T)
    gla_new = s_new_t.transpose(0, 2, 1).reshape(B, GLA_HEADS, GLA_DK, GLA_DV)

    n0 = ml_n.reshape(B, ML_HEADS, 1, ML_DH)
    m0 = jnp.broadcast_to(ml_m[:, :, None, None], (B, ML_HEADS, 1, ML_DH))
    cv0 = jnp.pad(ml_conv, ((0, 0), (8 - (ML_CONV - 1), 0), (0, 0)))
    mg, c_new, n_new, m_new, cv_new = _mlstm(ub, us, ust, p, ml_c, n0, m0, cv0, B, L, T)
    n_new = n_new.reshape(B, ML_HEADS, ML_DH)
    m_new = m_new[:, :, 0, 0]
    conv_new = cv_new[:, 8 - (ML_CONV - 1):, :]

    Tn = min(MB_BLOCK, n)
    qn, kn, km = _qknorm(ub, p["gq"], p["gk"], p["gmat"], n, Tn)
    if past is None:
        assert L % MB_BLOCK == 0
        bg = _moba_prompt(qn, kn, ub, km.reshape(B, L // MB_BLOCK, MB_W), B, L)
    else:
        cache_k, cache_v, page_table, layer = past
        bg = _moba_sample(qn, kn, ub, cache_k, cache_v, page_table, layer, B, L)
    bv0 = _BIG_OFF["b_v"]
    bk = kn.reshape(B, L, MB_HEADS, MB_DH)
    bv = ub[:, bv0:bv0 + MB_W].reshape(B, L, MB_HEADS, MB_DH)

    y = _merge(x2, ag, mg, bg, ub, p["gla_w_out"], p["ml_w_out"], p["mb_w_out"], p["w_out"])
    return y.reshape(B, L, D_MODEL), (gla_new, c_new, n_new, m_new, conv_new, bk, bv)


def kernel(x_prompt, x_sample, cache_k, cache_v, page_table, state_gla, state_mlstm_c, state_mlstm_n,
           state_mlstm_m, state_mlstm_conv, norm_g, w_in, gla_w_a2, gla_b_a, gla_norm, gla_w_out,
           ml_conv_w, ml_conv_b, ml_w_q, ml_w_k, ml_w_v, ml_b_i, ml_b_f, ml_norm, ml_skip, ml_w_out,
           mb_q_norm, mb_k_norm, mb_w_out, w_out):
    B = x_prompt.shape[0]
    depth = w_in.shape[0]
    dt = x_prompt.dtype
    weights = dict(norm_g=norm_g, w_in=w_in, gla_w_a2=gla_w_a2, gla_b_a=gla_b_a, gla_norm=gla_norm,
                   gla_w_out=gla_w_out, ml_conv_w=ml_conv_w, ml_conv_b=ml_conv_b, ml_w_q=ml_w_q,
                   ml_w_k=ml_w_k, ml_w_v=ml_w_v, ml_b_i=ml_b_i, ml_b_f=ml_b_f, ml_norm=ml_norm,
                   ml_skip=ml_skip, ml_w_out=ml_w_out, mb_q_norm=mb_q_norm, mb_k_norm=mb_k_norm,
                   mb_w_out=mb_w_out, w_out=w_out)
    xp, xs = x_prompt, x_sample
    p_list, s_list = [], []
    for l in range(depth):
        p = _prep_layer({k: v[l] for k, v in weights.items()})
        xp, sp = _layer(
            xp, p,
            jnp.zeros((B, GLA_HEADS, GLA_DK, GLA_DV), dt),
            jnp.zeros((B, ML_HEADS, ML_DH, ML_DH), dt),
            jnp.zeros((B, ML_HEADS, ML_DH), dt),
            jnp.zeros((B, ML_HEADS), dt),
            jnp.zeros((B, ML_CONV - 1, ML_W), dt),
            None)
        p_list.append(sp)
        xs, ss = _layer(xs, p, state_gla[l], state_mlstm_c[l], state_mlstm_n[l], state_mlstm_m[l],
                        state_mlstm_conv[l], (cache_k, cache_v, page_table, l))
        s_list.append(ss)
    p_out = [jnp.stack(z) for z in zip(*p_list)]
    s_out = [jnp.stack(z) for z in zip(*s_list)]
    return (xp, xs, *p_out, *s_out)
```
